```python
import math
import jax, jax.numpy as jnp
from jax import lax
import numpy as np

D_MODEL = 1024
BATCH = 8
SEQ = 4096
DEPTH = 4
DEC_BATCH = 16
DEC_SEQ = 4096
PAST_LEN = 128

N_META = 16
CHUNK = 128
PAD = CHUNK - N_META
SSD_HEADS = 16
SSD_HEAD_DIM = 64
SSD_WIDTH = SSD_HEADS * SSD_HEAD_DIM
SSD_GROUPS = 2
SSD_STATE = 128
CONV_WIDTH = 5
CONV_CH = SSD_WIDTH + 2 * SSD_GROUPS * SSD_STATE
RET_HEADS = 8
RET_QK_DIM = 64
RET_V_DIM = 128
RET_WIDTH = RET_HEADS * RET_V_DIM
ROPE_BASE = 10000.0
MIX_WIDTH = SSD_WIDTH + RET_WIDTH
D_FF = -(-8 * D_MODEL // (3 * 256)) * 256
EPS = 1e-6
SPLITS = (SSD_WIDTH, CONV_CH, 2 * SSD_HEADS, RET_HEADS * RET_QK_DIM, RET_HEADS * RET_QK_DIM, RET_WIDTH, RET_WIDTH)
N_IN = sum(SPLITS)
SPLIT_IDX = tuple(int(v) for v in np.cumsum(SPLITS)[:-1])

kernel_name = "hymba_style_ssd_retention_encoder"

F32 = jnp.float32


def _rms(x):
    x = x.astype(F32)
    return x * lax.rsqrt(jnp.mean(jnp.square(x), axis=-1, keepdims=True) + EPS)


def _rmsnorm(x, w):
    return (_rms(x) * w.astype(F32)).astype(x.dtype)


def _to_chunks(a):
    b, l = a.shape[:2]
    a = a.reshape((b, l // CHUNK, CHUNK) + a.shape[2:])
    return jnp.moveaxis(a, 1, 0)


def _from_chunks(a):
    a = jnp.moveaxis(a, 0, 1)
    return a.reshape((a.shape[0], a.shape[1] * a.shape[2]) + a.shape[3:])


def _dwconv(x, w):
    return lax.conv_general_dilated(x, w.astype(x.dtype)[:, None, :], window_strides=(1,),
                                    padding=[(CONV_WIDTH // 2, CONV_WIDTH // 2)],
                                    dimension_numbers=("NWC", "WIO", "NWC"),
                                    feature_group_count=x.shape[-1])


def _rotary(x, cos, sin):
    x1, x2 = jnp.split(x.astype(F32), 2, axis=-1)
    c, s = cos[None, :, None, :], sin[None, :, None, :]
    return jnp.concatenate([x1 * c - x2 * s, x2 * c + x1 * s], axis=-1)


def _ssd_scan(x, dt, A, B, C, include_diag):
    b, l, h, p = x.shape
    g, n = B.shape[2], B.shape[3]
    j = h // g
    dt = dt.astype(F32).reshape(b, l, g, j)
    dA = dt * A.astype(F32).reshape(g, j)
    xdt = x.astype(F32).reshape(b, l, g, j, p) * dt[..., None]
    tri = jnp.tril(jnp.ones((CHUNK, CHUNK), bool), 0 if include_diag else -1)[None, :, :, None, None]

    def step(state, inp):
        xdt_c, dA_c, B_c, C_c = inp
        acs = jnp.cumsum(dA_c, axis=1)
        seg = acs[:, :, None] - acs[:, None, :]
        Lm = jnp.exp(jnp.where(tri, seg, -jnp.inf))
        cb = jnp.einsum('blgn,bsgn->blsg', C_c, B_c)
        y = jnp.einsum('blsg,blsgj,bsgjp->blgjp', cb, Lm, xdt_c)
        y = y + jnp.einsum('blgn,bgjpn->blgjp', C_c, state) * jnp.exp(acs)[..., None]
        w_end = jnp.exp(acs[:, -1:] - acs)
        state = state * jnp.exp(acs[:, -1])[..., None, None] + jnp.einsum('bsgn,bsgj,bsgjp->bgjpn', B_c, w_end, xdt_c)
        return state, y

    init = jnp.zeros((b, g, j, p, n), F32)
    _, y = lax.scan(step, init, (_to_chunks(xdt), _to_chunks(dA), _to_chunks(B.astype(F32)), _to_chunks(C.astype(F32))))
    return _from_chunks(y).reshape(b, l, h, p)


def _retention_scan(q, k, v, log_gamma, include_diag):
    b, l, h, d = q.shape
    e = v.shape[-1]
    lg = log_gamma.astype(F32)
    idx = jnp.arange(CHUNK, dtype=F32)
    rel = idx[:, None] - idx[None, :]
    mask = rel >= 0 if include_diag else rel > 0
    dmat = jnp.where(mask[..., None], jnp.exp(jnp.where(mask, rel, 0.0)[..., None] * lg), 0.0)
    xi = jnp.exp((idx + 1.0)[:, None] * lg)
    zeta = jnp.exp((CHUNK - 1.0 - idx)[:, None] * lg)
    g_chunk = jnp.exp(CHUNK * lg)

    def step(R, inp):
        qc, kc, vc = inp
        s = jnp.einsum('blhd,bshd->blsh', qc, kc) * dmat
        o = jnp.einsum('blsh,bshe->blhe', s, vc) + jnp.einsum('blhd,bhde->blhe', qc * xi[None, :, :, None], R)
        R = R * g_chunk[:, None, None] + jnp.einsum('bshd,bshe->bhde', kc * zeta[None, :, :, None], vc)
        return R, o

    init = jnp.zeros((b, h, d, e), F32)
    _, o = lax.scan(step, init, (_to_chunks(q.astype(F32)), _to_chunks(k.astype(F32)), _to_chunks(v.astype(F32))))
    return _from_chunks(o)


def _layer(h, valid, cos, sin, g_mix_pre, g_mix_post, g_ffn_pre, g_ffn_post, w_in, conv_w, conv_b,
           dt_bias, a_log, d_skip, ssd_norm, ret_log_decay, w_out, w_gate, w_up, w_down):
    b, l, _ = h.shape
    flip = lambda a: jnp.flip(a, axis=1)
    u = _rmsnorm(h, g_mix_pre) * valid[None, :, None]
    proj = u @ w_in
    z, xbc, dt_raw, q, k, v, gate = jnp.split(proj, SPLIT_IDX, axis=-1)
    xbc = jax.nn.silu(_dwconv(xbc, conv_w) + conv_b)
    xs, bs, cs = jnp.split(xbc, [SSD_WIDTH, SSD_WIDTH + SSD_GROUPS * SSD_STATE], axis=-1)
    xs = xs.reshape(b, l, SSD_HEADS, SSD_HEAD_DIM)
    bs = bs.reshape(b, l, SSD_GROUPS, SSD_STATE)
    cs = cs.reshape(b, l, SSD_GROUPS, SSD_STATE)
    dt = jax.nn.softplus(dt_raw.reshape(b, l, 2, SSD_HEADS).astype(F32) + dt_bias.astype(F32)) * valid.astype(F32)[None, :, None, None]
    a = -jnp.exp(a_log.astype(F32))
    y_f = _ssd_scan(xs, dt[:, :, 0], a[0], bs, cs, True)
    y_b = flip(_ssd_scan(flip(xs), flip(dt[:, :, 1]), a[1], flip(bs), flip(cs), False))
    y = y_f + y_b + xs.astype(F32) * d_skip.astype(F32)[:, None]
    y = (y.reshape(b, l, SSD_WIDTH) * jax.nn.silu(z.astype(F32))).reshape(b, l, SSD_GROUPS, SSD_WIDTH // SSD_GROUPS)
    y_ssd = _rms(y).reshape(b, l, SSD_WIDTH) * ssd_norm.astype(F32)
    q = _rotary(q.reshape(b, l, RET_HEADS, RET_QK_DIM), cos, sin)
    k = _rotary(k.reshape(b, l, RET_HEADS, RET_QK_DIM), cos, sin) * (RET_QK_DIM ** -0.5)
    v = v.reshape(b, l, RET_HEADS, RET_V_DIM)
    r = _retention_scan(q, k, v, ret_log_decay[0], True) + flip(_retention_scan(flip(q), flip(k), flip(v), ret_log_decay[1], False))
    mu = jnp.mean(r, axis=-1, keepdims=True)
    r = (r - mu) * lax.rsqrt(jnp.mean(jnp.square(r - mu), axis=-1, keepdims=True) + EPS)
    y_ret = r.reshape(b, l, RET_WIDTH) * jax.nn.silu(gate.astype(F32))
    mix = jnp.concatenate([y_ssd, y_ret], axis=-1).astype(h.dtype) @ w_out
    h = h + _rmsnorm(mix, g_mix_post)
    f = _rmsnorm(h, g_ffn_pre)
    f = (jax.nn.silu(f @ w_gate) * (f @ w_up)) @ w_down
    return h + _rmsnorm(f, g_ffn_post)


def _encode(x, meta_tokens, norm_mix_pre, norm_mix_post, norm_ffn_pre, norm_ffn_post, w_in, conv_w, conv_b,
            dt_bias, a_log, d_skip, ssd_norm, ret_log_decay, w_out, w_gate, w_up, w_down):
    b = x.shape[0]
    lead = jnp.concatenate([jnp.zeros((b, PAD, D_MODEL), x.dtype),
                            jnp.broadcast_to(meta_tokens.astype(x.dtype), (b, N_META, D_MODEL))], axis=1)
    h = jnp.concatenate([lead, x], axis=1)
    L = h.shape[1]
    valid = jnp.concatenate([jnp.zeros((PAD,), x.dtype), jnp.ones((L - PAD,), x.dtype)])
    pos = jnp.arange(L, dtype=F32)
    inv_freq = ROPE_BASE ** (-jnp.arange(0, RET_QK_DIM, 2, dtype=F32) / RET_QK_DIM)
    ang = pos[:, None] * inv_freq[None, :]
    cos, sin = jnp.cos(ang), jnp.sin(ang)
    for i in range(DEPTH):
        h = _layer(h, valid, cos, sin, norm_mix_pre[i], norm_mix_post[i], norm_ffn_pre[i], norm_ffn_post[i],
                   w_in[i], conv_w[i], conv_b[i], dt_bias[i], a_log[i], d_skip[i], ssd_norm[i], ret_log_decay[i],
                   w_out[i], w_gate[i], w_up[i], w_down[i])
    return h[:, PAD + N_META:]


def setup_inputs(seed: int = 0) -> dict:
    key = jax.random.key(seed)
    ks = jax.random.split(key, 20)
    nrm = lambda k, shape, scale: jax.random.normal(k, shape, F32) * scale
    u_dt = jax.random.uniform(ks[10], (DEPTH, 2, SSD_HEADS), F32)
    dt0 = jnp.exp(u_dt * (math.log(0.1) - math.log(0.001)) + math.log(0.001))
    base_decay = jnp.log(1.0 - 2.0 ** (-5.0 - jnp.arange(RET_HEADS, dtype=F32)))
    return {
        "x_prompt": nrm(ks[0], (BATCH, SEQ, D_MODEL), 1.0),
        "x_sample": nrm(ks[1], (DEC_BATCH, DEC_SEQ, D_MODEL), 1.0),
        "meta_tokens": nrm(ks[2], (N_META, D_MODEL), 1.0),
        "norm_mix_pre": 1.0 + nrm(ks[3], (DEPTH, D_MODEL), 0.02),
        "norm_mix_post": 1.0 + nrm(ks[4], (DEPTH, D_MODEL), 0.02),
        "norm_ffn_pre": 1.0 + nrm(ks[5], (DEPTH, D_MODEL), 0.02),
        "norm_ffn_post": 1.0 + nrm(ks[6], (DEPTH, D_MODEL), 0.02),
        "w_in": nrm(ks[7], (DEPTH, D_MODEL, N_IN), D_MODEL ** -0.5),
        "conv_w": nrm(ks[8], (DEPTH, CONV_WIDTH, CONV_CH), CONV_WIDTH ** -0.5),
        "conv_b": nrm(ks[9], (DEPTH, CONV_CH), 0.02),
        "dt_bias": dt0 + jnp.log(-jnp.expm1(-dt0)),
        "a_log": jnp.log(jax.random.uniform(ks[11], (DEPTH, 2, SSD_HEADS), F32, 1.0, 16.0)),
        "d_skip": 1.0 + nrm(ks[12], (DEPTH, SSD_HEADS), 0.02),
        "ssd_norm": 1.0 + nrm(ks[13], (DEPTH, SSD_WIDTH), 0.02),
        "ret_log_decay": base_decay * (1.0 + nrm(ks[14], (DEPTH, 2, RET_HEADS), 0.05)),
        "w_out": nrm(ks[15], (DEPTH, MIX_WIDTH, D_MODEL), MIX_WIDTH ** -0.5),
        "w_gate": nrm(ks[16], (DEPTH, D_MODEL, D_FF), D_MODEL ** -0.5),
        "w_up": nrm(ks[17], (DEPTH, D_MODEL, D_FF), D_MODEL ** -0.5),
        "w_down": nrm(ks[18], (DEPTH, D_FF, D_MODEL), D_FF ** -0.5),
    }


def reference(x_prompt, x_sample, meta_tokens, norm_mix_pre, norm_mix_post, norm_ffn_pre, norm_ffn_post,
              w_in, conv_w, conv_b, dt_bias, a_log, d_skip, ssd_norm, ret_log_decay, w_out, w_gate, w_up, w_down):
    y_prompt = _encode(x_prompt, meta_tokens, norm_mix_pre, norm_mix_post, norm_ffn_pre, norm_ffn_post, w_in,
                       conv_w, conv_b, dt_bias, a_log, d_skip, ssd_norm, ret_log_decay, w_out, w_gate, w_up, w_down)
    y_sample = _encode(x_sample, meta_tokens, norm_mix_pre, norm_mix_post, norm_ffn_pre, norm_ffn_post, w_in,
                       conv_w, conv_b, dt_bias, a_log, d_skip, ssd_norm, ret_log_decay, w_out, w_gate, w_up, w_down)
    return (y_prompt, y_sample)
```

```python
import functools

import jax
import jax.numpy as jnp
from jax import lax
from jax.experimental import pallas as pl
from jax.experimental.pallas import tpu as pltpu

F32 = jnp.float32
BF16 = jnp.bfloat16

D_MODEL = 1024
N_META = 16
CHUNK = 128
PAD = CHUNK - N_META
SSD_HEADS = 16
SSD_HEAD_DIM = 64
SSD_WIDTH = SSD_HEADS * SSD_HEAD_DIM
SSD_GROUPS = 2
SSD_STATE = 128
GROUP_W = SSD_WIDTH // SSD_GROUPS
CONV_WIDTH = 5
BC_W = SSD_GROUPS * SSD_STATE
CONV_CH = SSD_WIDTH + 2 * BC_W
RET_HEADS = 8
RET_QK = 64
RET_V = 128
RET_PAIRS = RET_HEADS // 2
QK_W = RET_HEADS * RET_QK
RET_WIDTH = RET_HEADS * RET_V
MIX_WIDTH = SSD_WIDTH + RET_WIDTH
D_FF = 2816
ROPE_BASE = 10000.0
EPS = 1e-6
DT_W = 128
HALO = 8
LANES = 128
VMEM_LIMIT = 56 * 1024 * 1024

OFF_Z = 0
OFF_XBC = OFF_Z + SSD_WIDTH
OFF_Q = OFF_XBC + CONV_CH
OFF_K = OFF_Q + QK_W
OFF_V = OFF_K + QK_W
OFF_G = OFF_V + RET_WIDTH
OFF_DT = OFF_G + RET_WIDTH
N_PROJ = OFF_DT + DT_W


def _dot(a, b):
    return jnp.dot(a, b, preferred_element_type=F32)


def _dot_nt(a, b):
    return lax.dot_general(a, b, (((1,), (1,)), ((), ())), preferred_element_type=F32)


def _dot_tn(a, b):
    return lax.dot_general(a, b, (((0,), (0,)), ((), ())), preferred_element_type=F32)


def _silu(x):
    return x / (1.0 + jnp.exp(-x))


def _rms(x):
    return x * lax.rsqrt(jnp.mean(x * x, axis=-1, keepdims=True) + EPS)


def _split2(x):
    hi = x.astype(BF16)
    lo = (x - hi.astype(F32)).astype(BF16)
    return hi, lo


def _split3(x):
    hi = x.astype(BF16)
    r = x - hi.astype(F32)
    mid = r.astype(BF16)
    lo = (r - mid.astype(F32)).astype(BF16)
    return hi, mid, lo


def _rotate_half_pairs(x, lo_half):
    n = x.shape[-1]
    fwd = pltpu.roll(x, n - RET_QK // 2, 1)
    bwd = pltpu.roll(x, RET_QK // 2, 1)
    return jnp.where(lo_half, fwd, bwd)


def _inproj_kernel(h_ref, gain_ref, w_ref, cos_ref, sin_ref,
                   z_ref, xbc_ref, q_ref, k_ref, v_ref, g_ref, dt_ref, *, tm):
    t = pl.program_id(0)
    x = h_ref[0]
    u = _rms(x) * gain_ref[...]
    pos = t * tm + lax.broadcasted_iota(jnp.int32, (tm, 1), 0)
    u = jnp.where(pos >= PAD, u, 0.0).astype(BF16)

    z_ref[0] = _dot(u, w_ref[:, OFF_Z:OFF_XBC]).astype(z_ref.dtype)
    xbc_ref[0] = _dot(u, w_ref[:, OFF_XBC:OFF_Q]).astype(xbc_ref.dtype)
    v_ref[0] = _dot(u, w_ref[:, OFF_V:OFF_G]).astype(v_ref.dtype)
    g_ref[0] = _dot(u, w_ref[:, OFF_G:OFF_DT]).astype(g_ref.dtype)
    dt_ref[0] = _dot(u, w_ref[:, OFF_DT:N_PROJ])

    cos = cos_ref[...]
    sin = sin_ref[...]
    lane = lax.broadcasted_iota(jnp.int32, (tm, QK_W), 1)
    lo_half = (lane % RET_QK) < (RET_QK // 2)
    q = _dot(u, w_ref[:, OFF_Q:OFF_K])
    q_ref[0] = (q * cos + _rotate_half_pairs(q, lo_half) * sin).astype(q_ref.dtype)
    k = _dot(u, w_ref[:, OFF_K:OFF_V])
    k = (k * cos + _rotate_half_pairs(k, lo_half) * sin) * (RET_QK ** -0.5)
    k_ref[0] = k.astype(k_ref.dtype)


def _pick_tile(n, candidates):
    for c in candidates:
        if n % c == 0:
            return c
    raise ValueError(f"no tile for {n}")


def _const_spec(shape):
    nd = len(shape)
    return pl.BlockSpec(shape, lambda *_: (0,) * nd, pipeline_mode=pl.Buffered(1))


def _inproj(h, gain, w, cos, sin, act_dtype):
    rows, length, _ = h.shape
    tm = _pick_tile(length, (384, 256, 128))
    grid = (length // tm, rows)
    tok = lambda width: pl.BlockSpec((1, tm, width), lambda t, r: (r, t, 0))
    tab = pl.BlockSpec((tm, QK_W), lambda t, r: (t, 0))
    widths = (SSD_WIDTH, CONV_CH, QK_W, QK_W, RET_WIDTH, RET_WIDTH)
    out_shape = [jax.ShapeDtypeStruct((rows, length, wd), act_dtype) for wd in widths]
    out_shape.append(jax.ShapeDtypeStruct((rows, length, DT_W), F32))
    return pl.pallas_call(
        functools.partial(_inproj_kernel, tm=tm),
        grid=grid,
        in_specs=[tok(D_MODEL), _const_spec((1, D_MODEL)), _const_spec((D_MODEL, N_PROJ)), tab, tab],
        out_specs=[tok(wd) for wd in widths] + [tok(DT_W)],
        out_shape=out_shape,
        compiler_params=pltpu.CompilerParams(
            dimension_semantics=("arbitrary", "arbitrary"), vmem_limit_bytes=VMEM_LIMIT),
        name="inproj",
    )(h, gain, w, cos, sin)


def _conv_silu(cbuf_ref, xbc_ref, xprev_ref, xnext_ref, cw_ref, cb_ref, cc, nc, ncols):
    q = CHUNK
    prev = xprev_ref[0, :, :ncols].astype(F32)
    nxt = xnext_ref[0, :, :ncols].astype(F32)
    cbuf_ref[0:HALO, :ncols] = jnp.where(cc > 0, prev, 0.0)
    cbuf_ref[HALO:HALO + q, :ncols] = xbc_ref[0, :, :ncols].astype(F32)
    cbuf_ref[HALO + q:2 * HALO + q, :ncols] = jnp.where(cc < nc - 1, nxt, 0.0)
    acc = cb_ref[:, :ncols] + jnp.zeros((q, ncols), F32)
    for j in range(CONV_WIDTH):
        start = HALO - CONV_WIDTH // 2 + j
        acc = acc + cbuf_ref[start:start + q, :ncols] * cw_ref[j:j + 1, :ncols]
    return _silu(acc)


def _dt_terms(dt_ref, dtb_ref, alog_ref, tri_ref, cc):
    q = CHUNK
    raw = dt_ref[0] + dtb_ref[...]
    sp = jnp.maximum(raw, 0.0) + jnp.log1p(jnp.exp(-jnp.abs(raw)))
    pos = cc * q + lax.broadcasted_iota(jnp.int32, (q, DT_W), 0)
    dt = jnp.where(pos >= PAD, sp, 0.0)
    da = dt * (-jnp.exp(alog_ref[...]))
    tri = tri_ref[...]
    hi, mid, lo = _split3(da)
    acs = _dot(tri, hi) + _dot(tri, mid) + _dot(tri, lo)
    tot = acs[q - 1:q, :]
    return dt, da, acs, tot


def _expand(w, sel):
    hi, lo = _split2(w)
    return _dot(hi, sel) + _dot(lo, sel)


def _mixer_kernel(xbc_ref, xprev_ref, xnext_ref, dt_ref, k_ref, v_ref, z_ref, q_ref, g_ref,
                  cw_ref, cb_ref, dtb_ref, alog_ref, dskip_ref, snorm_ref,
                  dcomb_ref, xi_ref, zeta_ref, gch_ref, sel_ref, tri_ref,
                  out_ref,
                  s_fwd, s_bwd, r_fwd, r_bwd, s_saved, r_saved, cbuf, *, nc):
    q = CHUNK
    ph = pl.program_id(1)
    c = pl.program_id(2)
    col = lax.broadcasted_iota(jnp.int32, (q, DT_W), 1)
    is_fwd_col = col < SSD_HEADS
    row_lo = lax.broadcasted_iota(jnp.int32, (q, LANES), 0) < RET_QK

    def ret_state_update(r_state, kz, v, decay_ref, direction):
        for p in range(RET_PAIRS):
            u = _dot_tn(kz[:, p * LANES:(p + 1) * LANES], v[:, 2 * p * RET_V:(2 * p + 2) * RET_V])
            upd = jnp.where(row_lo, u[:, :RET_V], u[:, RET_V:])
            r_state[p] = r_state[p] * decay_ref[direction, p] + upd

    def ssd_state_update(s_state, bm, xs, decay_row):
        for g in range(SSD_GROUPS):
            gs = slice(g * GROUP_W, (g + 1) * GROUP_W)
            upd = _dot_tn(bm[:, g * SSD_STATE:(g + 1) * SSD_STATE], xs[:, gs])
            s_state[g] = s_state[g] * decay_row[:, gs] + upd

    @pl.when(ph == 0)
    def _():
        cc = nc - 1 - c

        @pl.when(c == 0)
        def _():
            s_bwd[...] = jnp.zeros_like(s_bwd)
            r_bwd[...] = jnp.zeros_like(r_bwd)

        s_saved[cc] = s_bwd[...].astype(BF16)
        r_saved[cc] = r_bwd[...].astype(BF16)

        xb = _conv_silu(cbuf, xbc_ref, xprev_ref, xnext_ref, cw_ref, cb_ref, cc, nc, SSD_WIDTH + BC_W)
        x = xb[:, :SSD_WIDTH]
        bm = xb[:, SSD_WIDTH:SSD_WIDTH + BC_W].astype(BF16)
        dt, da, acs, tot = _dt_terms(dt_ref, dtb_ref, alog_ref, tri_ref, cc)
        w2 = dt * jnp.exp(acs - da)
        e2 = _expand(w2, sel_ref[:, SSD_WIDTH:])
        dec = _expand(jnp.exp(tot) + jnp.zeros((HALO, DT_W), F32), sel_ref[:, SSD_WIDTH:])[0:1, :]
        ssd_state_update(s_bwd, bm, (x * e2).astype(BF16), dec)

        kz = (k_ref[0].astype(F32) * zeta_ref[1]).astype(BF16)
        ret_state_update(r_bwd, kz, v_ref[0].astype(BF16), gch_ref, 1)

    @pl.when(ph == 1)
    def _():
        cc = c

        @pl.when(c == 0)
        def _():
            s_fwd[...] = jnp.zeros_like(s_fwd)
            r_fwd[...] = jnp.zeros_like(r_fwd)

        xb = _conv_silu(cbuf, xbc_ref, xprev_ref, xnext_ref, cw_ref, cb_ref, cc, nc, CONV_CH)
        x = xb[:, :SSD_WIDTH]
        xh = x.astype(BF16)
        bm = xb[:, SSD_WIDTH:SSD_WIDTH + BC_W].astype(BF16)
        cm = xb[:, SSD_WIDTH + BC_W:].astype(BF16)
        dt, da, acs, tot = _dt_terms(dt_ref, dtb_ref, alog_ref, tri_ref, cc)
        exb = acs - da
        w1 = jnp.exp(jnp.where(is_fwd_col, acs, tot - exb))
        e1 = _expand(w1, sel_ref[...])
        w2 = dt * jnp.exp(tot - acs)
        e2 = _expand(w2, sel_ref[:, :SSD_WIDTH])

        acs_t = acs.T
        exb_t = exb.T
        dt_t = dt.T
        li = lax.broadcasted_iota(jnp.int32, (q, q), 0)
        si = lax.broadcasted_iota(jnp.int32, (q, q), 1)
        causal = li >= si
        lane = lax.broadcasted_iota(jnp.int32, (q, LANES), 1)
        first_head = lane < SSD_HEAD_DIM

        y_parts = []
        for g in range(SSD_GROUPS):
            cg = cm[:, g * SSD_STATE:(g + 1) * SSD_STATE]
            cb = _dot_nt(cg, bm[:, g * SSD_STATE:(g + 1) * SSD_STATE])
            heads_per_group = SSD_HEADS // SSD_GROUPS
            for pair in range(heads_per_group // 2):
                gmats = []
                for h in (g * heads_per_group + 2 * pair, g * heads_per_group + 2 * pair + 1):
                    hb = SSD_HEADS + h
                    arg = jnp.where(causal, acs[:, h:h + 1] - acs_t[h:h + 1, :],
                                    exb_t[hb:hb + 1, :] - exb[:, hb:hb + 1])
                    wgt = jnp.where(causal, dt_t[h:h + 1, :], dt_t[hb:hb + 1, :])
                    gmats.append((cb * (jnp.exp(arg) * wgt)).astype(BF16))
                lhs = jnp.concatenate(gmats, axis=1)
                p0 = (g * heads_per_group + 2 * pair) * SSD_HEAD_DIM
                xp = xh[:, p0:p0 + LANES]
                zero = jnp.zeros_like(xp)
                rhs = jnp.concatenate([jnp.where(first_head, xp, zero), jnp.where(first_head, zero, xp)], axis=0)
                y_parts.append(_dot(lhs, rhs))
        y = jnp.concatenate(y_parts, axis=1)

        s_b = s_saved[cc]
        inter = []
        for g in range(SSD_GROUPS):
            gs = slice(g * GROUP_W, (g + 1) * GROUP_W)
            cg = cm[:, g * SSD_STATE:(g + 1) * SSD_STATE]
            yf = _dot(cg, s_fwd[g].astype(BF16)) * e1[:, gs]
            gsb = slice(SSD_WIDTH + g * GROUP_W, SSD_WIDTH + (g + 1) * GROUP_W)
            yb = _dot(cg, s_b[g]) * e1[:, gsb]
            inter.append(yf + yb)
        y = y + jnp.concatenate(inter, axis=1) + x * dskip_ref[...]
        y = y * _silu(z_ref[0].astype(F32))
        ssd_out = []
        for g in range(SSD_GROUPS):
            gs = slice(g * GROUP_W, (g + 1) * GROUP_W)
            ssd_out.append(_rms(y[:, gs]) * snorm_ref[:, gs])
        out_ref[0, :, :SSD_WIDTH] = jnp.concatenate(ssd_out, axis=1).astype(out_ref.dtype)

        ssd_state_update(s_fwd, bm, (x * e2).astype(BF16), e1[q - 1:q, :SSD_WIDTH])

        qv = q_ref[0].astype(F32)
        kv = k_ref[0].astype(F32)
        vv = v_ref[0].astype(BF16)
        gate = _silu(g_ref[0].astype(F32))
        qb = qv.astype(BF16)
        kb = kv.astype(BF16)
        q_xf = (qv * xi_ref[0]).astype(BF16)
        q_xb = (qv * xi_ref[1]).astype(BF16)
        r_b = r_saved[cc]
        zero_qk = jnp.zeros((q, LANES), BF16)
        for p in range(RET_PAIRS):
            ps = slice(p * LANES, (p + 1) * LANES)
            kp = kb[:, ps]
            r_f_p = r_fwd[p].astype(BF16)
            for half in range(2):
                h = 2 * p + half
                keep = first_head if half == 0 else jnp.logical_not(first_head)
                s = _dot_nt(jnp.where(keep, qb[:, ps], zero_qk), kp) * dcomb_ref[h]
                vh = vv[:, h * RET_V:(h + 1) * RET_V]
                o = _dot(s.astype(BF16), vh)
                o = o + _dot(jnp.where(keep, q_xf[:, ps], zero_qk), r_f_p)
                o = o + _dot(jnp.where(keep, q_xb[:, ps], zero_qk), r_b[p])
                mu = jnp.mean(o, axis=-1, keepdims=True)
                d = o - mu
                o = d * lax.rsqrt(jnp.mean(d * d, axis=-1, keepdims=True) + EPS)
                hs = slice(SSD_WIDTH + h * RET_V, SSD_WIDTH + (h + 1) * RET_V)
                out_ref[0, :, hs] = (o * gate[:, h * RET_V:(h + 1) * RET_V]).astype(out_ref.dtype)

        kz = (kv * zeta_ref[0]).astype(BF16)
        ret_state_update(r_fwd, kz, vv, gch_ref, 0)


def _mixer(z, xbc, qa, ka, va, ga, dt, tabs):
    rows, length, _ = xbc.shape
    nc = length // CHUNK
    nh = length // HALO
    hpc = CHUNK // HALO

    def cidx(ph, c):
        return c + (1 - ph) * (nc - 1 - 2 * c)

    both = lambda width: pl.BlockSpec((1, CHUNK, width), lambda r, ph, c: (r, cidx(ph, c), 0))
    fwd_only = lambda width: pl.BlockSpec((1, CHUNK, width), lambda r, ph, c: (r, ph * c, 0))
    prev_spec = pl.BlockSpec((1, HALO, CONV_CH), lambda r, ph, c: (r, jnp.maximum(cidx(ph, c) * hpc - 1, 0), 0))
    next_spec = pl.BlockSpec((1, HALO, CONV_CH),
                             lambda r, ph, c: (r, jnp.minimum((cidx(ph, c) + 1) * hpc, nh - 1), 0))
    in_specs = [both(CONV_CH), prev_spec, next_spec, both(DT_W), both(QK_W), both(RET_WIDTH),
                fwd_only(SSD_WIDTH), fwd_only(QK_W), fwd_only(RET_WIDTH)]
    in_specs += [_const_spec(t.shape) for t in tabs]
    scratch = [
        pltpu.VMEM((SSD_GROUPS, SSD_STATE, GROUP_W), F32),
        pltpu.VMEM((SSD_GROUPS, SSD_STATE, GROUP_W), F32),
        pltpu.VMEM((RET_PAIRS, 2 * RET_QK, RET_V), F32),
        pltpu.VMEM((RET_PAIRS, 2 * RET_QK, RET_V), F32),
        pltpu.VMEM((nc, SSD_GROUPS, SSD_STATE, GROUP_W), BF16),
        pltpu.VMEM((nc, RET_PAIRS, 2 * RET_QK, RET_V), BF16),
        pltpu.VMEM((CHUNK + 2 * HALO, CONV_CH), F32),
    ]
    return pl.pallas_call(
        functools.partial(_mixer_kernel, nc=nc),
        grid=(rows, 2, nc),
        in_specs=in_specs,
        out_specs=pl.BlockSpec((1, CHUNK, MIX_WIDTH), lambda r, ph, c: (r, ph * c, 0)),
        out_shape=jax.ShapeDtypeStruct((rows, length, MIX_WIDTH), BF16),
        scratch_shapes=scratch,
        compiler_params=pltpu.CompilerParams(
            dimension_semantics=("arbitrary", "arbitrary", "arbitrary"), vmem_limit_bytes=VMEM_LIMIT),
        name="mixer",
    )(xbc, xbc, xbc, dt, ka, va, z, qa, ga, *tabs)


FF_CHUNK = 512


def _ffn_kernel(mix_ref, h_ref, g_post_ref, g_pre_ref, g_fpost_ref, wo_ref, wg_ref, wu_ref, wd_ref, out_ref):
    m = _dot(mix_ref[...], wo_ref[...])
    h1 = h_ref[...] + _rms(m) * g_post_ref[...]
    f = (_rms(h1) * g_pre_ref[...]).astype(BF16)
    acc = jnp.zeros(h1.shape, F32)
    for c0 in range(0, D_FF, FF_CHUNK):
        c1 = min(c0 + FF_CHUNK, D_FF)
        gt = _dot(f, wg_ref[:, c0:c1])
        up = _dot(f, wu_ref[:, c0:c1])
        acc = acc + _dot((_silu(gt) * up).astype(BF16), wd_ref[c0:c1, :])
    out_ref[...] = h1 + _rms(acc) * g_fpost_ref[...]


def _ffn(mix, h, g_post, g_pre, g_fpost, wo, wg, wu, wd):
    n = h.shape[0]
    tm = _pick_tile(n, (512, 256, 128))
    tok = lambda width: pl.BlockSpec((tm, width), lambda i: (i, 0))
    return pl.pallas_call(
        _ffn_kernel,
        grid=(n // tm,),
        in_specs=[tok(MIX_WIDTH), tok(D_MODEL), _const_spec((1, D_MODEL)), _const_spec((1, D_MODEL)),
                  _const_spec((1, D_MODEL)), _const_spec(wo.shape), _const_spec(wg.shape),
                  _const_spec(wu.shape), _const_spec(wd.shape)],
        out_specs=tok(D_MODEL),
        out_shape=jax.ShapeDtypeStruct((n, D_MODEL), F32),
        compiler_params=pltpu.CompilerParams(dimension_semantics=("arbitrary",), vmem_limit_bytes=VMEM_LIMIT),
        name="ffn",
    )(mix, h, g_post, g_pre, g_fpost, wo, wg, wu, wd)


def _rope_tables(length):
    pos = jnp.arange(length, dtype=F32)
    inv_freq = ROPE_BASE ** (-jnp.arange(0, RET_QK, 2, dtype=F32) / RET_QK)
    ang = pos[:, None] * inv_freq[None, :]
    cos, sin = jnp.cos(ang), jnp.sin(ang)
    cos_full = jnp.tile(jnp.concatenate([cos, cos], axis=1), (1, RET_HEADS))
    sin_full = jnp.tile(jnp.concatenate([-sin, sin], axis=1), (1, RET_HEADS))
    return cos_full, sin_full


def _selection_matrix():
    j = jnp.arange(DT_W)[:, None]
    lane = jnp.arange(2 * SSD_WIDTH)[None, :]
    src = jnp.where(lane < SSD_WIDTH, lane // SSD_HEAD_DIM, SSD_HEADS + (lane - SSD_WIDTH) // SSD_HEAD_DIM)
    return (j == src).astype(BF16)


def _retention_tables(log_decay):
    lg_f = log_decay[0].astype(F32)
    lg_b = log_decay[1].astype(F32)
    idx = jnp.arange(CHUNK, dtype=F32)
    rel = idx[:, None] - idx[None, :]
    causal = rel >= 0
    d_f = jnp.exp(jnp.where(causal, rel, 0.0)[None] * lg_f[:, None, None])
    d_b = jnp.exp(jnp.where(causal, 0.0, -rel)[None] * lg_b[:, None, None])
    dcomb = jnp.where(causal[None], d_f, d_b)
    per_lane = lambda t: jnp.repeat(t, RET_QK, axis=1)
    xi_f = per_lane(jnp.exp((idx + 1.0)[:, None] * lg_f))
    xi_b = per_lane(jnp.exp((CHUNK - idx)[:, None] * lg_b))
    zeta_f = per_lane(jnp.exp((CHUNK - 1.0 - idx)[:, None] * lg_f))
    zeta_b = per_lane(jnp.exp(idx[:, None] * lg_b))
    xi = jnp.stack([xi_f, xi_b])
    zeta = jnp.stack([zeta_f, zeta_b])
    gch = jnp.exp(CHUNK * jnp.stack([lg_f, lg_b]))
    gch = jnp.repeat(gch, RET_QK, axis=1).reshape(2, RET_PAIRS, 2 * RET_QK, 1)
    gch = jnp.broadcast_to(gch, (2, RET_PAIRS, 2 * RET_QK, RET_V))
    return dcomb, xi, zeta, gch


def _pad_cols(a, width):
    return jnp.pad(a, ((0, 0), (0, width - a.shape[1])))


def _layer_tables(conv_w, conv_b, dt_bias, a_log, d_skip, ssd_norm, ret_log_decay, sel, tri):
    cw = jnp.pad(conv_w.astype(F32), ((0, HALO - CONV_WIDTH), (0, 0)))
    cb = conv_b.astype(F32)[None, :]
    dtb = _pad_cols(dt_bias.astype(F32).reshape(1, 2 * SSD_HEADS), DT_W)
    alog = _pad_cols(a_log.astype(F32).reshape(1, 2 * SSD_HEADS), DT_W)
    dskip = jnp.repeat(d_skip.astype(F32), SSD_HEAD_DIM)[None, :]
    snorm = ssd_norm.astype(F32)[None, :]
    dcomb, xi, zeta, gch = _retention_tables(ret_log_decay)
    return (cw, cb, dtb, alog, dskip, snorm, dcomb, xi, zeta, gch, sel, tri)


def _reorder_w_in(w):
    z, xbc, dt, q, k, v, g = jnp.split(
        w, [SSD_WIDTH, SSD_WIDTH + CONV_CH, SSD_WIDTH + CONV_CH + 2 * SSD_HEADS,
            SSD_WIDTH + CONV_CH + 2 * SSD_HEADS + QK_W, SSD_WIDTH + CONV_CH + 2 * SSD_HEADS + 2 * QK_W,
            SSD_WIDTH + CONV_CH + 2 * SSD_HEADS + 2 * QK_W + RET_WIDTH], axis=1)
    return jnp.concatenate([z, xbc, q, k, v, g, _pad_cols(dt, DT_W)], axis=1).astype(BF16)


ACT_DTYPE = F32


def kernel(x_prompt, x_sample, meta_tokens, norm_mix_pre, norm_mix_post, norm_ffn_pre, norm_ffn_post, w_in, conv_w, conv_b, dt_bias, a_log, d_skip, ssd_norm, ret_log_decay, w_out, w_gate, w_up, w_down):
    depth = w_in.shape[0]
    x = jnp.concatenate([x_prompt, x_sample], axis=0)
    rows, seq, _ = x.shape
    lead = jnp.concatenate([jnp.zeros((rows, PAD, D_MODEL), x.dtype),
                            jnp.broadcast_to(meta_tokens.astype(x.dtype), (rows, N_META, D_MODEL))], axis=1)
    h = jnp.concatenate([lead, x], axis=1)
    length = h.shape[1]
    cos, sin = _rope_tables(length)
    sel = _selection_matrix()
    idx = jnp.arange(CHUNK)
    tri = (idx[:, None] >= idx[None, :]).astype(BF16)
    row = lambda a: a.astype(F32)[None, :]
    for i in range(depth):
        tabs = _layer_tables(conv_w[i], conv_b[i], dt_bias[i], a_log[i], d_skip[i], ssd_norm[i],
                             ret_log_decay[i], sel, tri)
        z, xbc, qa, ka, va, ga, dt = _inproj(h, row(norm_mix_pre[i]), _reorder_w_in(w_in[i]), cos, sin, ACT_DTYPE)
        mix = _mixer(z, xbc, qa, ka, va, ga, dt, tabs)
        h = _ffn(mix.reshape(rows * length, MIX_WIDTH), h.reshape(rows * length, D_MODEL),
                 row(norm_mix_post[i]), row(norm_ffn_pre[i]), row(norm_ffn_post[i]),
                 w_out[i].astype(BF16), w_gate[i].astype(BF16), w_up[i].astype(BF16), w_down[i].astype(BF16))
        h = h.reshape(rows, length, D_MODEL)
    y = h[:, PAD + N_META:]
    nb = x_prompt.shape[0]
    return (y[:nb], y[nb:])
```

```python
import functools

import jax
import jax.numpy as jnp
from jax import lax
from jax.experimental import pallas as pl
from jax.experimental.pallas import tpu as pltpu

F32 = jnp.float32
BF16 = jnp.bfloat16

D_MODEL = 1024
N_META = 16
CHUNK = 128
PAD = CHUNK - N_META
SSD_HEADS = 16
SSD_HEAD_DIM = 64
SSD_WIDTH = SSD_HEADS * SSD_HEAD_DIM
SSD_GROUPS = 2
SSD_STATE = 128
GROUP_W = SSD_WIDTH // SSD_GROUPS
CONV_WIDTH = 5
BC_W = SSD_GROUPS * SSD_STATE
CONV_CH = SSD_WIDTH + 2 * BC_W
RET_HEADS = 8
RET_QK = 64
RET_V = 128
RET_PAIRS = RET_HEADS // 2
QK_W = RET_HEADS * RET_QK
RET_WIDTH = RET_HEADS * RET_V
MIX_WIDTH = SSD_WIDTH + RET_WIDTH
D_FF = 2816
ROPE_BASE = 10000.0
EPS = 1e-6
DT_W = 128
HALO = 8
LANES = 128
VMEM_LIMIT = 56 * 1024 * 1024

OFF_Z = 0
OFF_XBC = OFF_Z + SSD_WIDTH
OFF_Q = OFF_XBC + CONV_CH
OFF_K = OFF_Q + QK_W
OFF_V = OFF_K + QK_W
OFF_G = OFF_V + RET_WIDTH
OFF_DT = OFF_G + RET_WIDTH
N_PROJ = OFF_DT + DT_W


def _dot(a, b):
    return jnp.dot(a, b, preferred_element_type=F32)


def _dot_nt(a, b):
    return lax.dot_general(a, b, (((1,), (1,)), ((), ())), preferred_element_type=F32)


def _dot_tn(a, b):
    return lax.dot_general(a, b, (((0,), (0,)), ((), ())), preferred_element_type=F32)


def _silu(x):
    half = 0.5 * x
    return half + half * jnp.tanh(half)


def _rms(x):
    return x * lax.rsqrt(jnp.mean(x * x, axis=-1, keepdims=True) + EPS)


def _split2(x):
    hi = x.astype(BF16)
    lo = (x - hi.astype(F32)).astype(BF16)
    return hi, lo


def _split3(x):
    hi = x.astype(BF16)
    r = x - hi.astype(F32)
    mid = r.astype(BF16)
    lo = (r - mid.astype(F32)).astype(BF16)
    return hi, mid, lo


def _pick_tile(n, candidates):
    for c in candidates:
        if n % c == 0:
            return c
    raise ValueError(f"no tile for {n}")


def _const_spec(shape):
    nd = len(shape)
    return pl.BlockSpec(shape, lambda *_: (0,) * nd, pipeline_mode=pl.Buffered(1))


def _rotate_half_pairs(x, lo_half):
    n = x.shape[-1]
    fwd = pltpu.roll(x, n - RET_QK // 2, 1)
    bwd = pltpu.roll(x, RET_QK // 2, 1)
    return jnp.where(lo_half, fwd, bwd)


N_STRIPS = CONV_CH // LANES


def _conv_strip(c, win, act, xact_ref, cw_ref, cb_ref, tm):
    stream = tm // HALO
    half = CONV_WIDTH // 2
    ls = slice(c * LANES, (c + 1) * LANES)
    wts = [jnp.broadcast_to(cw_ref[j:j + 1, ls], (HALO, LANES)) for j in range(CONV_WIDTH)]
    bias = jnp.broadcast_to(cb_ref[:, ls], (HALO, LANES))
    taps = [win[c, pl.ds(HALO - half + d, HALO, stride=stream), :] for d in range(CONV_WIDTH - 1)]
    for i in range(stream):
        taps.append(win[c, pl.ds(HALO + half + i, HALO, stride=stream), :])
        acc = bias
        for j in range(CONV_WIDTH):
            acc = acc + taps[j] * wts[j]
        act[c, pl.ds(i, HALO, stride=stream), :] = _silu(acc)
        taps.pop(0)
    xact_ref[0, :, ls] = act[c].astype(xact_ref.dtype)


PIECE = 2 * LANES


def _inproj_kernel(h_ref, gain_ref, w_ref, cos_ref, sin_ref, cw_ref, cb_ref, dtb_ref,
                   zs_ref, xact_ref, q_ref, k_ref, v_ref, gs_ref, dt_ref, win, act, *, tm, nt):
    t = pl.program_id(1)

    @pl.when(t == 0)
    def _():
        win[...] = jnp.zeros_like(win)

    @pl.when(t < nt)
    def _():
        x = h_ref[0]
        u = _rms(x) * gain_ref[...]
        pos = t * tm + lax.broadcasted_iota(jnp.int32, (tm, 1), 0)
        valid = pos >= PAD
        u = jnp.where(valid, u, 0.0).astype(BF16)

        def proj(lo, width):
            return _dot(u, w_ref[:, lo:lo + width])

        xbc_pieces = {}

        def xbc_piece(p):
            xbc_pieces[p] = proj(OFF_XBC + p * PIECE, PIECE)

        def conv(c):
            strip = xbc_pieces[c // 2][:, (c % 2) * LANES:(c % 2 + 1) * LANES]
            win[c, HALO + tm:, :] = strip[:HALO, :]
            _conv_strip(c, win, act, xact_ref, cw_ref, cb_ref, tm)
            win[c, :HALO, :] = win[c, tm:tm + HALO, :]
            win[c, HALO:HALO + tm, :] = strip

        def z_piece(i):
            cols = slice(i * PIECE, (i + 1) * PIECE)
            zs_ref[0, :, cols] = _silu(proj(OFF_Z + i * PIECE, PIECE)).astype(zs_ref.dtype)

        def g_piece(i):
            cols = slice(i * PIECE, (i + 1) * PIECE)
            gs_ref[0, :, cols] = _silu(proj(OFF_G + i * PIECE, PIECE)).astype(gs_ref.dtype)

        def v_piece(i):
            cols = slice(i * PIECE, (i + 1) * PIECE)
            v_ref[0, :, cols] = proj(OFF_V + i * PIECE, PIECE).astype(v_ref.dtype)

        def dt_piece():
            raw = proj(OFF_DT, DT_W) + dtb_ref[...]
            sp = jnp.maximum(raw, 0.0) + jnp.log1p(jnp.exp(-jnp.abs(raw)))
            dt_ref[0] = jnp.where(valid, sp, 0.0)

        lane = lax.broadcasted_iota(jnp.int32, (tm, QK_W), 1)
        lo_half = (lane % RET_QK) < (RET_QK // 2)

        def rotary(lo, out_ref, scale):
            a = proj(lo, QK_W)
            a = a * cos_ref[...] + _rotate_half_pairs(a, lo_half) * sin_ref[...]
            out_ref[0] = (a * scale).astype(out_ref.dtype) if scale != 1.0 else a.astype(out_ref.dtype)

        xbc_piece(0)
        z_piece(0)
        conv(0)
        xbc_piece(1)
        conv(1)
        z_piece(1)
        conv(2)
        xbc_piece(2)
        conv(3)
        z_piece(2)
        conv(4)
        xbc_piece(3)
        conv(5)
        z_piece(3)
        conv(6)
        xbc_piece(4)
        conv(7)
        dt_piece()
        conv(8)
        xbc_piece(5)
        conv(9)
        rotary(OFF_Q, q_ref, 1.0)
        conv(10)
        rotary(OFF_K, k_ref, RET_QK ** -0.5)
        conv(11)
        for i in range(RET_WIDTH // PIECE):
            g_piece(i)
        for i in range(RET_WIDTH // PIECE):
            v_piece(i)

    @pl.when(t == nt)
    def _():
        win[:, HALO + tm:, :] = jnp.zeros((N_STRIPS, HALO, LANES), F32)
        for c in range(N_STRIPS):
            _conv_strip(c, win, act, xact_ref, cw_ref, cb_ref, tm)


def _inproj(h, gain, w, cos, sin, cw, cb, dtb):
    rows, length, _ = h.shape
    tm = _pick_tile(length, (528, 352, 256, 128))
    nt = length // tm
    cur = lambda t: jnp.minimum(t, nt - 1)
    tok = lambda width: pl.BlockSpec((1, tm, width), lambda r, t: (r, cur(t), 0))
    lag = pl.BlockSpec((1, tm, CONV_CH), lambda r, t: (r, jnp.maximum(t - 1, 0), 0))
    tab = pl.BlockSpec((tm, QK_W), lambda r, t: (cur(t), 0))
    out_widths = (SSD_WIDTH, CONV_CH, QK_W, QK_W, RET_WIDTH, RET_WIDTH)
    out_shape = [jax.ShapeDtypeStruct((rows, length, wd), BF16) for wd in out_widths]
    out_shape.append(jax.ShapeDtypeStruct((rows, length, DT_W), F32))
    out_specs = [tok(SSD_WIDTH), lag, tok(QK_W), tok(QK_W), tok(RET_WIDTH), tok(RET_WIDTH), tok(DT_W)]
    return pl.pallas_call(
        functools.partial(_inproj_kernel, tm=tm, nt=nt),
        grid=(rows, nt + 1),
        in_specs=[tok(D_MODEL), _const_spec((1, D_MODEL)), _const_spec((D_MODEL, N_PROJ)), tab, tab,
                  _const_spec(cw.shape), _const_spec(cb.shape), _const_spec(dtb.shape)],
        out_specs=out_specs,
        out_shape=out_shape,
        scratch_shapes=[pltpu.VMEM((N_STRIPS, tm + 2 * HALO, LANES), F32),
                        pltpu.VMEM((N_STRIPS, tm, LANES), F32)],
        compiler_params=pltpu.CompilerParams(
            dimension_semantics=("arbitrary", "arbitrary"), vmem_limit_bytes=VMEM_LIMIT),
        name="inproj",
    )(h, gain, w, cos, sin, cw, cb, dtb)


def _dt_terms(dt_ref, alog_ref, tri_ref):
    q = CHUNK
    dt = dt_ref[0]
    da = dt * (-jnp.exp(alog_ref[...]))
    tri = tri_ref[...]
    hi, mid, lo = _split3(da)
    acs = _dot(tri, hi) + _dot(tri, mid) + _dot(tri, lo)
    tot = acs[q - 1:q, :]
    return dt, da, acs, tot


def _mixer_kernel(xact_ref, dt_ref, k_ref, v_ref, zs_ref, q_ref, gs_ref,
                  alog_ref, dskip_ref, snorm_ref, dcomb_ref, xi_ref, zeta_ref, gch_ref, sel_ref, tri_ref,
                  out_ref,
                  s_fwd, s_bwd, r_fwd, r_bwd, s_saved, r_saved, *, nc):
    q = CHUNK
    ph = pl.program_id(1)
    c = pl.program_id(2)
    col = lax.broadcasted_iota(jnp.int32, (q, DT_W), 1)
    is_fwd_col = col < SSD_HEADS
    pr = lax.broadcasted_iota(jnp.int32, (2 * RET_QK, 2 * RET_V), 0)
    pc = lax.broadcasted_iota(jnp.int32, (2 * RET_QK, 2 * RET_V), 1)
    own_block = (pr < RET_QK) == (pc < RET_V)

    def ret_state_update(r_state, kz, v, direction):
        for p in range(RET_PAIRS):
            u = _dot_tn(kz[:, p * LANES:(p + 1) * LANES], v[:, 2 * p * RET_V:(2 * p + 2) * RET_V])
            g = gch_ref[direction, p]
            decayed = r_state[p] * jnp.concatenate([g, g], axis=1)
            r_state[p] = decayed + jnp.where(own_block, u, 0.0)

    def ssd_state_update(s_state, bm, xs, decay_row):
        for g in range(SSD_GROUPS):
            gs = slice(g * GROUP_W, (g + 1) * GROUP_W)
            upd = _dot_tn(bm[:, g * SSD_STATE:(g + 1) * SSD_STATE], xs[:, gs])
            s_state[g] = s_state[g] * decay_row[:, gs] + upd

    @pl.when(ph == 0)
    def _():
        cc = nc - 1 - c

        @pl.when(c == 0)
        def _():
            s_bwd[...] = jnp.zeros_like(s_bwd)
            r_bwd[...] = jnp.zeros_like(r_bwd)

        s_saved[cc] = s_bwd[...].astype(BF16)
        r_saved[cc] = r_bwd[...].astype(BF16)

        x = xact_ref[0, :, :SSD_WIDTH]
        bm = xact_ref[0, :, SSD_WIDTH:SSD_WIDTH + BC_W]
        dt, da, acs, tot = _dt_terms(dt_ref, alog_ref, tri_ref)
        w2 = dt * jnp.exp(acs - da)
        dec = jnp.exp(tot) + jnp.zeros((HALO, DT_W), F32)
        dec_hi = dec.astype(BF16).astype(F32)
        lhs = jnp.concatenate([w2, dec_hi, dec - dec_hi], axis=0).astype(BF16)
        e = _dot(lhs, sel_ref[:DT_W, SSD_WIDTH:])
        xs = x * e[:q].astype(BF16)
        ssd_state_update(s_bwd, bm, xs, e[q:q + 1] + e[q + HALO:q + HALO + 1])

        ret_state_update(r_bwd, k_ref[0] * zeta_ref[1], v_ref[0], 1)

    @pl.when(ph == 1)
    def _():
        cc = c

        @pl.when(c == 0)
        def _():
            s_fwd[...] = jnp.zeros_like(s_fwd)
            r_fwd[...] = jnp.zeros_like(r_fwd)

        x = xact_ref[0, :, :SSD_WIDTH]
        bm = xact_ref[0, :, SSD_WIDTH:SSD_WIDTH + BC_W]
        cm = xact_ref[0, :, SSD_WIDTH + BC_W:]
        dt, da, acs, tot = _dt_terms(dt_ref, alog_ref, tri_ref)
        exb = acs - da
        w1 = jnp.exp(jnp.where(is_fwd_col, acs, tot - exb))
        w1_hi, w1_lo = _split2(w1)
        e1 = _dot(jnp.concatenate([w1_hi, w1_lo], axis=1), sel_ref[...])
        w2 = (dt * jnp.exp(tot - acs)).astype(BF16)
        e2 = _dot(w2, sel_ref[:DT_W, :SSD_WIDTH]).astype(BF16)

        acs_t = acs.T
        exb_t = exb.T
        dt_t = dt.T
        li = lax.broadcasted_iota(jnp.int32, (q, q), 0)
        si = lax.broadcasted_iota(jnp.int32, (q, q), 1)
        causal = li >= si
        lane = lax.broadcasted_iota(jnp.int32, (q, LANES), 1)
        first_head = lane < SSD_HEAD_DIM

        y_parts = []
        heads_per_group = SSD_HEADS // SSD_GROUPS
        for g in range(SSD_GROUPS):
            cg = cm[:, g * SSD_STATE:(g + 1) * SSD_STATE]
            cb = _dot_nt(cg, bm[:, g * SSD_STATE:(g + 1) * SSD_STATE])
            for pair in range(heads_per_group // 2):
                h0 = g * heads_per_group + 2 * pair
                gmats = []
                for h in (h0, h0 + 1):
                    hb = SSD_HEADS + h
                    arg = jnp.where(causal, acs[:, h:h + 1] - acs_t[h:h + 1, :],
                                    exb_t[hb:hb + 1, :] - exb[:, hb:hb + 1])
                    wgt = jnp.where(causal, dt_t[h:h + 1, :], dt_t[hb:hb + 1, :])
                    gmats.append((cb * (jnp.exp(arg) * wgt)).astype(BF16))
                lhs = jnp.concatenate(gmats, axis=1)
                xp = x[:, h0 * SSD_HEAD_DIM:h0 * SSD_HEAD_DIM + LANES]
                zero = jnp.zeros_like(xp)
                rhs = jnp.concatenate([jnp.where(first_head, xp, zero), jnp.where(first_head, zero, xp)], axis=0)
                y_parts.append(_dot(lhs, rhs))
        y = jnp.concatenate(y_parts, axis=1)

        s_b = s_saved[cc]
        inter = []
        for g in range(SSD_GROUPS):
            gs = slice(g * GROUP_W, (g + 1) * GROUP_W)
            cg = cm[:, g * SSD_STATE:(g + 1) * SSD_STATE]
            yf = _dot(cg, s_fwd[g].astype(BF16)) * e1[:, gs]
            gsb = slice(SSD_WIDTH + g * GROUP_W, SSD_WIDTH + (g + 1) * GROUP_W)
            yb = _dot(cg, s_b[g]) * e1[:, gsb]
            inter.append(yf + yb)
        y = y + jnp.concatenate(inter, axis=1) + x.astype(F32) * dskip_ref[...]
        y = y * zs_ref[0].astype(F32)
        ssd_out = []
        for g in range(SSD_GROUPS):
            gs = slice(g * GROUP_W, (g + 1) * GROUP_W)
            ssd_out.append(_rms(y[:, gs]) * snorm_ref[:, gs])
        out_ref[0, :, :SSD_WIDTH] = jnp.concatenate(ssd_out, axis=1).astype(out_ref.dtype)

        ssd_state_update(s_fwd, bm, x * e2, e1[q - 1:q, :SSD_WIDTH])

        qa = q_ref[0]
        ka = k_ref[0]
        va = v_ref[0]
        q_xf = qa * xi_ref[0]
        q_xb = qa * xi_ref[1]
        r_b = r_saved[cc]
        lane_v = lax.broadcasted_iota(jnp.int32, (q, 2 * RET_V), 1)
        first_v = lane_v < RET_V
        for p in range(RET_PAIRS):
            ps = slice(p * LANES, (p + 1) * LANES)
            kp = ka[:, ps]
            zk = jnp.zeros_like(kp)
            k_sep = jnp.concatenate([jnp.where(first_head, kp, zk), jnp.where(first_head, zk, kp)], axis=0)
            s = _dot_nt(qa[:, ps], k_sep) * dcomb_ref[p]
            vp = va[:, 2 * p * RET_V:(2 * p + 2) * RET_V]
            zv = jnp.zeros_like(vp)
            v_sep = jnp.concatenate([jnp.where(first_v, vp, zv), jnp.where(first_v, zv, vp)], axis=0)
            o = _dot(s.astype(BF16), v_sep)
            states = jnp.concatenate([r_fwd[p].astype(BF16), r_b[p]], axis=0)
            o = o + _dot(jnp.concatenate([q_xf[:, ps], q_xb[:, ps]], axis=1), states)
            for half in range(2):
                h = 2 * p + half
                oh = o[:, half * RET_V:(half + 1) * RET_V]
                d = oh - jnp.mean(oh, axis=-1, keepdims=True)
                oh = d * lax.rsqrt(jnp.mean(d * d, axis=-1, keepdims=True) + EPS)
                hs = slice(SSD_WIDTH + h * RET_V, SSD_WIDTH + (h + 1) * RET_V)
                gate = gs_ref[0, :, h * RET_V:(h + 1) * RET_V].astype(F32)
                out_ref[0, :, hs] = (oh * gate).astype(out_ref.dtype)

        ret_state_update(r_fwd, ka * zeta_ref[0], va, 0)


def _mixer(zs, xact, qa, ka, va, gs, dt, tabs):
    rows, length, _ = xact.shape
    nc = length // CHUNK

    def cidx(ph, c):
        return c + (1 - ph) * (nc - 1 - 2 * c)

    both = lambda width: pl.BlockSpec((1, CHUNK, width), lambda r, ph, c: (r, cidx(ph, c), 0))
    fwd_only = lambda width: pl.BlockSpec((1, CHUNK, width), lambda r, ph, c: (r, ph * c, 0))
    in_specs = [both(CONV_CH), both(DT_W), both(QK_W), both(RET_WIDTH),
                fwd_only(SSD_WIDTH), fwd_only(QK_W), fwd_only(RET_WIDTH)]
    in_specs += [_const_spec(t.shape) for t in tabs]
    scratch = [
        pltpu.VMEM((SSD_GROUPS, SSD_STATE, GROUP_W), F32),
        pltpu.VMEM((SSD_GROUPS, SSD_STATE, GROUP_W), F32),
        pltpu.VMEM((RET_PAIRS, 2 * RET_QK, 2 * RET_V), F32),
        pltpu.VMEM((RET_PAIRS, 2 * RET_QK, 2 * RET_V), F32),
        pltpu.VMEM((nc, SSD_GROUPS, SSD_STATE, GROUP_W), BF16),
        pltpu.VMEM((nc, RET_PAIRS, 2 * RET_QK, 2 * RET_V), BF16),
    ]
    return pl.pallas_call(
        functools.partial(_mixer_kernel, nc=nc),
        grid=(rows, 2, nc),
        in_specs=in_specs,
        out_specs=pl.BlockSpec((1, CHUNK, MIX_WIDTH), lambda r, ph, c: (r, ph * c, 0)),
        out_shape=jax.ShapeDtypeStruct((rows, length, MIX_WIDTH), BF16),
        scratch_shapes=scratch,
        compiler_params=pltpu.CompilerParams(
            dimension_semantics=("arbitrary", "arbitrary", "arbitrary"), vmem_limit_bytes=VMEM_LIMIT),
        name="mixer",
    )(xact, dt, ka, va, zs, qa, gs, *tabs)


FF_CHUNK = 512


def _ffn_kernel(mix_ref, h_ref, g_post_ref, g_pre_ref, g_fpost_ref, wo_ref, wg_ref, wu_ref, wd_ref, out_ref):
    m = _dot(mix_ref[...], wo_ref[...])
    h1 = h_ref[...] + _rms(m) * g_post_ref[...]
    f = (_rms(h1) * g_pre_ref[...]).astype(BF16)
    acc = jnp.zeros(h1.shape, F32)
    for c0 in range(0, D_FF, FF_CHUNK):
        c1 = min(c0 + FF_CHUNK, D_FF)
        gt = _dot(f, wg_ref[:, c0:c1])
        up = _dot(f, wu_ref[:, c0:c1])
        acc = acc + _dot((_silu(gt) * up).astype(BF16), wd_ref[c0:c1, :])
    out_ref[...] = h1 + _rms(acc) * g_fpost_ref[...]


def _ffn(mix, h, g_post, g_pre, g_fpost, wo, wg, wu, wd):
    n = h.shape[0]
    tm = _pick_tile(n, (512, 256, 128))
    tok = lambda width: pl.BlockSpec((tm, width), lambda i: (i, 0))
    return pl.pallas_call(
        _ffn_kernel,
        grid=(n // tm,),
        in_specs=[tok(MIX_WIDTH), tok(D_MODEL), _const_spec((1, D_MODEL)), _const_spec((1, D_MODEL)),
                  _const_spec((1, D_MODEL)), _const_spec(wo.shape), _const_spec(wg.shape),
                  _const_spec(wu.shape), _const_spec(wd.shape)],
        out_specs=tok(D_MODEL),
        out_shape=jax.ShapeDtypeStruct((n, D_MODEL), F32),
        compiler_params=pltpu.CompilerParams(dimension_semantics=("arbitrary",), vmem_limit_bytes=VMEM_LIMIT),
        name="ffn",
    )(mix, h, g_post, g_pre, g_fpost, wo, wg, wu, wd)


def _rope_tables(length):
    pos = jnp.arange(length, dtype=F32)
    inv_freq = ROPE_BASE ** (-jnp.arange(0, RET_QK, 2, dtype=F32) / RET_QK)
    ang = pos[:, None] * inv_freq[None, :]
    cos, sin = jnp.cos(ang), jnp.sin(ang)
    cos_full = jnp.tile(jnp.concatenate([cos, cos], axis=1), (1, RET_HEADS))
    sin_full = jnp.tile(jnp.concatenate([-sin, sin], axis=1), (1, RET_HEADS))
    return cos_full, sin_full


def _selection_matrix():
    j = jnp.arange(DT_W)[:, None]
    lane = jnp.arange(2 * SSD_WIDTH)[None, :]
    src = jnp.where(lane < SSD_WIDTH, lane // SSD_HEAD_DIM, SSD_HEADS + (lane - SSD_WIDTH) // SSD_HEAD_DIM)
    sel = (j == src).astype(BF16)
    return jnp.concatenate([sel, sel], axis=0)


def _retention_tables(log_decay):
    lg_f = log_decay[0].astype(F32)
    lg_b = log_decay[1].astype(F32)
    idx = jnp.arange(CHUNK, dtype=F32)
    rel = idx[:, None] - idx[None, :]
    causal = rel >= 0
    d_f = jnp.exp(jnp.where(causal, rel, 0.0)[None] * lg_f[:, None, None])
    d_b = jnp.exp(jnp.where(causal, 0.0, -rel)[None] * lg_b[:, None, None])
    dcomb = jnp.where(causal[None], d_f, d_b)
    dcomb = dcomb.reshape(RET_PAIRS, 2, CHUNK, CHUNK).transpose(0, 2, 1, 3).reshape(RET_PAIRS, CHUNK, 2 * CHUNK)
    per_lane = lambda t: jnp.repeat(t, RET_QK, axis=1)
    xi_f = per_lane(jnp.exp((idx + 1.0)[:, None] * lg_f))
    xi_b = per_lane(jnp.exp((CHUNK - idx)[:, None] * lg_b))
    zeta_f = per_lane(jnp.exp((CHUNK - 1.0 - idx)[:, None] * lg_f))
    zeta_b = per_lane(jnp.exp(idx[:, None] * lg_b))
    xi = jnp.stack([xi_f, xi_b]).astype(BF16)
    zeta = jnp.stack([zeta_f, zeta_b]).astype(BF16)
    gch = jnp.exp(CHUNK * jnp.stack([lg_f, lg_b]))
    gch = jnp.repeat(gch, RET_QK, axis=1).reshape(2, RET_PAIRS, 2 * RET_QK, 1)
    gch = jnp.broadcast_to(gch, (2, RET_PAIRS, 2 * RET_QK, RET_V))
    return dcomb, xi, zeta, gch


def _pad_cols(a, width):
    return jnp.pad(a, ((0, 0), (0, width - a.shape[1])))


def _mixer_tables(a_log, d_skip, ssd_norm, ret_log_decay, sel, tri):
    alog = _pad_cols(a_log.astype(F32).reshape(1, 2 * SSD_HEADS), DT_W)
    dskip = jnp.repeat(d_skip.astype(F32), SSD_HEAD_DIM)[None, :]
    snorm = ssd_norm.astype(F32)[None, :]
    dcomb, xi, zeta, gch = _retention_tables(ret_log_decay)
    return (alog, dskip, snorm, dcomb, xi, zeta, gch, sel, tri)


def _reorder_w_in(w):
    z, xbc, dt, q, k, v, g = jnp.split(
        w, [SSD_WIDTH, SSD_WIDTH + CONV_CH, SSD_WIDTH + CONV_CH + 2 * SSD_HEADS,
            SSD_WIDTH + CONV_CH + 2 * SSD_HEADS + QK_W, SSD_WIDTH + CONV_CH + 2 * SSD_HEADS + 2 * QK_W,
            SSD_WIDTH + CONV_CH + 2 * SSD_HEADS + 2 * QK_W + RET_WIDTH], axis=1)
    return jnp.concatenate([z, xbc, q, k, v, g, _pad_cols(dt, DT_W)], axis=1).astype(BF16)


def kernel(x_prompt, x_sample, meta_tokens, norm_mix_pre, norm_mix_post, norm_ffn_pre, norm_ffn_post, w_in, conv_w, conv_b, dt_bias, a_log, d_skip, ssd_norm, ret_log_decay, w_out, w_gate, w_up, w_down):
    depth = w_in.shape[0]
    x = jnp.concatenate([x_prompt, x_sample], axis=0)
    rows, seq, _ = x.shape
    lead = jnp.concatenate([jnp.zeros((rows, PAD, D_MODEL), x.dtype),
                            jnp.broadcast_to(meta_tokens.astype(x.dtype), (rows, N_META, D_MODEL))], axis=1)
    h = jnp.concatenate([lead, x], axis=1)
    length = h.shape[1]
    cos, sin = _rope_tables(length)
    sel = _selection_matrix()
    idx = jnp.arange(CHUNK)
    tri = (idx[:, None] >= idx[None, :]).astype(BF16)
    row = lambda a: a.astype(F32)[None, :]
    for i in range(depth):
        cw = jnp.pad(conv_w[i].astype(F32), ((0, HALO - CONV_WIDTH), (0, 0)))
        dtb = _pad_cols(dt_bias[i].astype(F32).reshape(1, 2 * SSD_HEADS), DT_W)
        zs, xact, qa, ka, va, gs, dt = _inproj(h, row(norm_mix_pre[i]), _reorder_w_in(w_in[i]), cos, sin,
                                               cw, row(conv_b[i]), dtb)
        tabs = _mixer_tables(a_log[i], d_skip[i], ssd_norm[i], ret_log_decay[i], sel, tri)
        mix = _mixer(zs, xact, qa, ka, va, gs, dt, tabs)
        h = _ffn(mix.reshape(rows * length, MIX_WIDTH), h.reshape(rows * length, D_MODEL),
                 row(norm_mix_post[i]), row(norm_ffn_pre[i]), row(norm_ffn_post[i]),
                 w_out[i].astype(BF16), w_gate[i].astype(BF16), w_up[i].astype(BF16), w_down[i].astype(BF16))
        h = h.reshape(rows, length, D_MODEL)
    y = h[:, PAD + N_META:]
    nb = x_prompt.shape[0]
    return (y[:nb], y[nb:])
```

```python
import functools

import jax
import jax.numpy as jnp
from jax import lax
from jax.experimental import pallas as pl
from jax.experimental.pallas import tpu as pltpu

F32 = jnp.float32
BF16 = jnp.bfloat16

D_MODEL = 1024
N_META = 16
CHUNK = 128
PAD = CHUNK - N_META
SSD_HEADS = 16
SSD_HEAD_DIM = 64
SSD_WIDTH = SSD_HEADS * SSD_HEAD_DIM
SSD_GROUPS = 2
SSD_STATE = 128
GROUP_W = SSD_WIDTH // SSD_GROUPS
CONV_WIDTH = 5
BC_W = SSD_GROUPS * SSD_STATE
CONV_CH = SSD_WIDTH + 2 * BC_W
RET_HEADS = 8
RET_QK = 64
RET_V = 128
RET_PAIRS = RET_HEADS // 2
QK_W = RET_HEADS * RET_QK
RET_WIDTH = RET_HEADS * RET_V
MIX_WIDTH = SSD_WIDTH + RET_WIDTH
D_FF = 2816
ROPE_BASE = 10000.0
EPS = 1e-6
DT_W = 128
HALO = 8
LANES = 128
VMEM_LIMIT = 56 * 1024 * 1024

OFF_Z = 0
OFF_XBC = OFF_Z + SSD_WIDTH
OFF_Q = OFF_XBC + CONV_CH
OFF_K = OFF_Q + QK_W
OFF_V = OFF_K + QK_W
OFF_G = OFF_V + RET_WIDTH
OFF_DT = OFF_G + RET_WIDTH
N_PROJ = OFF_DT + DT_W


def _dot(a, b):
    return jnp.dot(a, b, preferred_element_type=F32)


def _dot_nt(a, b):
    return lax.dot_general(a, b, (((1,), (1,)), ((), ())), preferred_element_type=F32)


def _dot_tn(a, b):
    return lax.dot_general(a, b, (((0,), (0,)), ((), ())), preferred_element_type=F32)


def _silu(x):
    half = 0.5 * x
    return half + half * jnp.tanh(half)


def _rms(x):
    return x * lax.rsqrt(jnp.mean(x * x, axis=-1, keepdims=True) + EPS)


def _split2(x):
    hi = x.astype(BF16)
    lo = (x - hi.astype(F32)).astype(BF16)
    return hi, lo


def _split3(x):
    hi = x.astype(BF16)
    r = x - hi.astype(F32)
    mid = r.astype(BF16)
    lo = (r - mid.astype(F32)).astype(BF16)
    return hi, mid, lo


def _pick_tile(n, candidates):
    for c in candidates:
        if n % c == 0:
            return c
    raise ValueError(f"no tile for {n}")


def _const_spec(shape):
    nd = len(shape)
    return pl.BlockSpec(shape, lambda *_: (0,) * nd, pipeline_mode=pl.Buffered(1))


def _rotate_half_pairs(x, lo_half):
    n = x.shape[-1]
    fwd = pltpu.roll(x, n - RET_QK // 2, 1)
    bwd = pltpu.roll(x, RET_QK // 2, 1)
    return jnp.where(lo_half, fwd, bwd)


N_STRIPS = CONV_CH // LANES


def _conv_strip(c, win, act, xact_ref, cw_ref, cb_ref, tm):
    stream = tm // HALO
    half = CONV_WIDTH // 2
    ls = slice(c * LANES, (c + 1) * LANES)
    wts = [jnp.broadcast_to(cw_ref[j:j + 1, ls], (HALO, LANES)) for j in range(CONV_WIDTH)]
    bias = jnp.broadcast_to(cb_ref[:, ls], (HALO, LANES))
    taps = [win[c, pl.ds(HALO - half + d, HALO, stride=stream), :] for d in range(CONV_WIDTH - 1)]
    for i in range(stream):
        taps.append(win[c, pl.ds(HALO + half + i, HALO, stride=stream), :])
        acc = bias
        for j in range(CONV_WIDTH):
            acc = acc + taps[j] * wts[j]
        act[c, pl.ds(i, HALO, stride=stream), :] = _silu(acc)
        taps.pop(0)
    xact_ref[0, :, ls] = act[c].astype(xact_ref.dtype)


PIECE = 2 * LANES


def _inproj_kernel(h_ref, gain_ref, w_ref, cos_ref, sin_ref, cw_ref, cb_ref, dtb_ref,
                   zs_ref, xact_ref, q_ref, k_ref, v_ref, gs_ref, dt_ref, win, act, *, tm, nt):
    t = pl.program_id(1)

    @pl.when(t == 0)
    def _():
        win[...] = jnp.zeros_like(win)

    @pl.when(t < nt)
    def _():
        x = h_ref[0]
        u = _rms(x) * gain_ref[...]
        pos = t * tm + lax.broadcasted_iota(jnp.int32, (tm, 1), 0)
        valid = pos >= PAD
        u = jnp.where(valid, u, 0.0).astype(BF16)

        def proj(lo, width):
            return _dot(u, w_ref[:, lo:lo + width])

        xbc_pieces = {}

        def xbc_piece(p):
            xbc_pieces[p] = proj(OFF_XBC + p * PIECE, PIECE)

        def conv(c):
            strip = xbc_pieces[c // 2][:, (c % 2) * LANES:(c % 2 + 1) * LANES]
            win[c, HALO + tm:, :] = strip[:HALO, :]
            _conv_strip(c, win, act, xact_ref, cw_ref, cb_ref, tm)
            win[c, :HALO, :] = win[c, tm:tm + HALO, :]
            win[c, HALO:HALO + tm, :] = strip

        def z_piece(i):
            cols = slice(i * PIECE, (i + 1) * PIECE)
            zs_ref[0, :, cols] = _silu(proj(OFF_Z + i * PIECE, PIECE)).astype(zs_ref.dtype)

        def g_piece(i):
            cols = slice(i * PIECE, (i + 1) * PIECE)
            gs_ref[0, :, cols] = _silu(proj(OFF_G + i * PIECE, PIECE)).astype(gs_ref.dtype)

        def v_piece(i):
            cols = slice(i * PIECE, (i + 1) * PIECE)
            v_ref[0, :, cols] = proj(OFF_V + i * PIECE, PIECE).astype(v_ref.dtype)

        def dt_piece():
            raw = proj(OFF_DT, DT_W) + dtb_ref[...]
            sp = jnp.maximum(raw, 0.0) + jnp.log1p(jnp.exp(-jnp.abs(raw)))
            dt_ref[0] = jnp.where(valid, sp, 0.0)

        lane = lax.broadcasted_iota(jnp.int32, (tm, QK_W), 1)
        lo_half = (lane % RET_QK) < (RET_QK // 2)

        def rotary(lo, out_ref, scale):
            a = proj(lo, QK_W)
            a = a * cos_ref[...] + _rotate_half_pairs(a, lo_half) * sin_ref[...]
            out_ref[0] = (a * scale).astype(out_ref.dtype) if scale != 1.0 else a.astype(out_ref.dtype)

        xbc_piece(0)
        z_piece(0)
        conv(0)
        xbc_piece(1)
        conv(1)
        z_piece(1)
        conv(2)
        xbc_piece(2)
        conv(3)
        z_piece(2)
        conv(4)
        xbc_piece(3)
        conv(5)
        z_piece(3)
        conv(6)
        xbc_piece(4)
        conv(7)
        dt_piece()
        conv(8)
        xbc_piece(5)
        conv(9)
        rotary(OFF_Q, q_ref, 1.0)
        conv(10)
        rotary(OFF_K, k_ref, RET_QK ** -0.5)
        conv(11)
        for i in range(RET_WIDTH // PIECE):
            g_piece(i)
        for i in range(RET_WIDTH // PIECE):
            v_piece(i)

    @pl.when(t == nt)
    def _():
        win[:, HALO + tm:, :] = jnp.zeros((N_STRIPS, HALO, LANES), F32)
        for c in range(N_STRIPS):
            _conv_strip(c, win, act, xact_ref, cw_ref, cb_ref, tm)


def _inproj(h, gain, w, cos, sin, cw, cb, dtb):
    rows, length, _ = h.shape
    tm = _pick_tile(length, (528, 352, 256, 128))
    nt = length // tm
    cur = lambda t: jnp.minimum(t, nt - 1)
    tok = lambda width: pl.BlockSpec((1, tm, width), lambda r, t: (r, cur(t), 0))
    lag = pl.BlockSpec((1, tm, CONV_CH), lambda r, t: (r, jnp.maximum(t - 1, 0), 0))
    tab = pl.BlockSpec((tm, QK_W), lambda r, t: (cur(t), 0))
    out_widths = (SSD_WIDTH, CONV_CH, QK_W, QK_W, RET_WIDTH, RET_WIDTH)
    out_shape = [jax.ShapeDtypeStruct((rows, length, wd), BF16) for wd in out_widths]
    out_shape.append(jax.ShapeDtypeStruct((rows, length, DT_W), F32))
    out_specs = [tok(SSD_WIDTH), lag, tok(QK_W), tok(QK_W), tok(RET_WIDTH), tok(RET_WIDTH), tok(DT_W)]
    return pl.pallas_call(
        functools.partial(_inproj_kernel, tm=tm, nt=nt),
        grid=(rows, nt + 1),
        in_specs=[tok(D_MODEL), _const_spec((1, D_MODEL)), _const_spec((D_MODEL, N_PROJ)), tab, tab,
                  _const_spec(cw.shape), _const_spec(cb.shape), _const_spec(dtb.shape)],
        out_specs=out_specs,
        out_shape=out_shape,
        scratch_shapes=[pltpu.VMEM((N_STRIPS, tm + 2 * HALO, LANES), F32),
                        pltpu.VMEM((N_STRIPS, tm, LANES), F32)],
        compiler_params=pltpu.CompilerParams(
            dimension_semantics=("arbitrary", "arbitrary"), vmem_limit_bytes=VMEM_LIMIT),
        name="inproj",
    )(h, gain, w, cos, sin, cw, cb, dtb)


CHUNKS_PER_STEP = 3


def _dt_terms(dt, alog_ref, tri_ref):
    da = dt * (-jnp.exp(alog_ref[...]))
    tri = tri_ref[...]
    hi, mid, lo = _split3(da)
    acs = _dot(tri, hi) + _dot(tri, mid) + _dot(tri, lo)
    return da, acs, acs[CHUNK - 1:CHUNK, :]


def _mixer_kernel(xact_ref, dt_ref, k_ref, v_ref, zs_ref, q_ref, gs_ref,
                  alog_ref, dskip_ref, snorm_ref, dcomb_ref, xi_ref, zeta_ref, gch_ref, sel_ref, tri_ref,
                  out_ref,
                  s_fwd, s_bwd, r_fwd, r_bwd, s_saved, r_saved, *, nb, group):
    q = CHUNK
    ph = pl.program_id(1)
    b = pl.program_id(2)
    col = lax.broadcasted_iota(jnp.int32, (q, DT_W), 1)
    is_fwd_col = col < SSD_HEADS
    pr = lax.broadcasted_iota(jnp.int32, (2 * RET_QK, 2 * RET_V), 0)
    pc = lax.broadcasted_iota(jnp.int32, (2 * RET_QK, 2 * RET_V), 1)
    own_block = (pr < RET_QK) == (pc < RET_V)
    heads_per_group = SSD_HEADS // SSD_GROUPS

    def rows(j):
        return slice(j * q, (j + 1) * q)

    def ssd_update_terms(bm, xs):
        return [_dot_tn(bm[:, g * SSD_STATE:(g + 1) * SSD_STATE], xs[:, g * GROUP_W:(g + 1) * GROUP_W])
                for g in range(SSD_GROUPS)]

    def ret_update_terms(kz, v):
        return [jnp.where(own_block,
                          _dot_tn(kz[:, p * LANES:(p + 1) * LANES], v[:, 2 * p * RET_V:(2 * p + 2) * RET_V]), 0.0)
                for p in range(RET_PAIRS)]

    def advance_ssd(s_state, decay_row, upd):
        for g in range(SSD_GROUPS):
            s_state[g] = s_state[g] * decay_row[:, g * GROUP_W:(g + 1) * GROUP_W] + upd[g]

    def advance_ret(r_state, direction, upd):
        for p in range(RET_PAIRS):
            gamma = gch_ref[direction, p]
            r_state[p] = r_state[p] * jnp.concatenate([gamma, gamma], axis=1) + upd[p]

    @pl.when(ph == 0)
    def _():
        base = (nb - 1 - b) * group

        @pl.when(b == 0)
        def _():
            s_bwd[...] = jnp.zeros_like(s_bwd)
            r_bwd[...] = jnp.zeros_like(r_bwd)

        order = list(range(group - 1, -1, -1))
        terms = {}
        for j in order:
            dt = dt_ref[0, rows(j), :]
            da, acs, tot = _dt_terms(dt, alog_ref, tri_ref)
            terms[j] = (dt, da, acs, tot)
        ssd_upd, ssd_dec, ret_upd = {}, {}, {}
        for j in order:
            dt, da, acs, tot = terms[j]
            w2 = dt * jnp.exp(acs - da)
            dec = jnp.exp(tot) + jnp.zeros((HALO, DT_W), F32)
            dec_hi = dec.astype(BF16).astype(F32)
            lhs = jnp.concatenate([w2, dec_hi, dec - dec_hi], axis=0).astype(BF16)
            e = _dot(lhs, sel_ref[:DT_W, SSD_WIDTH:])
            xs = xact_ref[0, rows(j), :SSD_WIDTH] * e[:q].astype(BF16)
            ssd_dec[j] = e[q:q + 1] + e[q + HALO:q + HALO + 1]
            ssd_upd[j] = ssd_update_terms(xact_ref[0, rows(j), SSD_WIDTH:SSD_WIDTH + BC_W], xs)
        for j in order:
            ret_upd[j] = ret_update_terms(k_ref[0, rows(j), :] * zeta_ref[1], v_ref[0, rows(j), :])
        for j in order:
            s_saved[base + j] = s_bwd[...].astype(BF16)
            advance_ssd(s_bwd, ssd_dec[j], ssd_upd[j])
            r_saved[base + j] = r_bwd[...].astype(BF16)
            advance_ret(r_bwd, 1, ret_upd[j])

    @pl.when(ph == 1)
    def _():
        base = b * group

        @pl.when(b == 0)
        def _():
            s_fwd[...] = jnp.zeros_like(s_fwd)
            r_fwd[...] = jnp.zeros_like(r_fwd)

        li = lax.broadcasted_iota(jnp.int32, (q, q), 0)
        si = lax.broadcasted_iota(jnp.int32, (q, q), 1)
        causal = li >= si
        lane = lax.broadcasted_iota(jnp.int32, (q, LANES), 1)
        first_head = lane < SSD_HEAD_DIM
        lane_v = lax.broadcasted_iota(jnp.int32, (q, 2 * RET_V), 1)
        first_v = lane_v < RET_V
        chunks = list(range(group))

        terms, e1s, e2s, trans = {}, {}, {}, {}
        for j in chunks:
            dt = dt_ref[0, rows(j), :]
            da, acs, tot = _dt_terms(dt, alog_ref, tri_ref)
            terms[j] = (dt, acs, acs - da, tot)
        for j in chunks:
            dt, acs, exb, tot = terms[j]
            w1_hi, w1_lo = _split2(jnp.exp(jnp.where(is_fwd_col, acs, tot - exb)))
            e1s[j] = _dot(jnp.concatenate([w1_hi, w1_lo], axis=1), sel_ref[...])
            w2 = (dt * jnp.exp(tot - acs)).astype(BF16)
            e2s[j] = _dot(w2, sel_ref[:DT_W, :SSD_WIDTH]).astype(BF16)
            trans[j] = (acs.T, exb.T, dt.T)

        ssd_upd, ret_upd = {}, {}
        for j in chunks:
            x = xact_ref[0, rows(j), :SSD_WIDTH]
            ssd_upd[j] = ssd_update_terms(xact_ref[0, rows(j), SSD_WIDTH:SSD_WIDTH + BC_W], x * e2s[j])
            ret_upd[j] = ret_update_terms(k_ref[0, rows(j), :] * zeta_ref[0], v_ref[0, rows(j), :])

        s_in, r_in = {}, {}
        for j in chunks:
            s_in[j] = [s_fwd[g].astype(BF16) for g in range(SSD_GROUPS)]
            advance_ssd(s_fwd, e1s[j][q - 1:q, :SSD_WIDTH], ssd_upd[j])
            r_in[j] = [r_fwd[p].astype(BF16) for p in range(RET_PAIRS)]
            advance_ret(r_fwd, 0, ret_upd[j])

        y_intra = {}
        for j in chunks:
            dt, acs, exb, tot = terms[j]
            acs_t, exb_t, dt_t = trans[j]
            x = xact_ref[0, rows(j), :SSD_WIDTH]
            parts = []
            for g in range(SSD_GROUPS):
                bm = xact_ref[0, rows(j), SSD_WIDTH + g * SSD_STATE:SSD_WIDTH + (g + 1) * SSD_STATE]
                cg = xact_ref[0, rows(j), SSD_WIDTH + BC_W + g * SSD_STATE:SSD_WIDTH + BC_W + (g + 1) * SSD_STATE]
                cb = _dot_nt(cg, bm)
                for pair in range(heads_per_group // 2):
                    h0 = g * heads_per_group + 2 * pair
                    gmats = []
                    for h in (h0, h0 + 1):
                        hb = SSD_HEADS + h
                        arg = jnp.where(causal, acs[:, h:h + 1] - acs_t[h:h + 1, :],
                                        exb_t[hb:hb + 1, :] - exb[:, hb:hb + 1])
                        wgt = jnp.where(causal, dt_t[h:h + 1, :], dt_t[hb:hb + 1, :])
                        gmats.append((cb * (jnp.exp(arg) * wgt)).astype(BF16))
                    xp = x[:, h0 * SSD_HEAD_DIM:h0 * SSD_HEAD_DIM + LANES]
                    zero = jnp.zeros_like(xp)
                    rhs = jnp.concatenate([jnp.where(first_head, xp, zero), jnp.where(first_head, zero, xp)], axis=0)
                    parts.append(_dot(jnp.concatenate(gmats, axis=1), rhs))
            y_intra[j] = parts

        for j in chunks:
            x = xact_ref[0, rows(j), :SSD_WIDTH]
            s_b = s_saved[base + j]
            for g in range(SSD_GROUPS):
                gs = slice(g * GROUP_W, (g + 1) * GROUP_W)
                gsb = slice(SSD_WIDTH + g * GROUP_W, SSD_WIDTH + (g + 1) * GROUP_W)
                cg = xact_ref[0, rows(j), SSD_WIDTH + BC_W + g * SSD_STATE:SSD_WIDTH + BC_W + (g + 1) * SSD_STATE]
                y = jnp.concatenate(y_intra[j][g * 4:(g + 1) * 4], axis=1)
                y = y + _dot(cg, s_in[j][g]) * e1s[j][:, gs] + _dot(cg, s_b[g]) * e1s[j][:, gsb]
                y = y + x[:, gs].astype(F32) * dskip_ref[:, gs]
                y = y * zs_ref[0, rows(j), gs].astype(F32)
                out_ref[0, rows(j), gs] = (_rms(y) * snorm_ref[:, gs]).astype(out_ref.dtype)

        for j in chunks:
            qa = q_ref[0, rows(j), :]
            ka = k_ref[0, rows(j), :]
            va = v_ref[0, rows(j), :]
            q_xf = qa * xi_ref[0]
            q_xb = qa * xi_ref[1]
            r_b = r_saved[base + j]
            for p in range(RET_PAIRS):
                ps = slice(p * LANES, (p + 1) * LANES)
                kp = ka[:, ps]
                zk = jnp.zeros_like(kp)
                k_sep = jnp.concatenate([jnp.where(first_head, kp, zk), jnp.where(first_head, zk, kp)], axis=0)
                s = _dot_nt(qa[:, ps], k_sep) * dcomb_ref[p]
                vp = va[:, 2 * p * RET_V:(2 * p + 2) * RET_V]
                zv = jnp.zeros_like(vp)
                v_sep = jnp.concatenate([jnp.where(first_v, vp, zv), jnp.where(first_v, zv, vp)], axis=0)
                o = _dot(s.astype(BF16), v_sep)
                states = jnp.concatenate([r_in[j][p], r_b[p]], axis=0)
                o = o + _dot(jnp.concatenate([q_xf[:, ps], q_xb[:, ps]], axis=1), states)
                for half in range(2):
                    h = 2 * p + half
                    oh = o[:, half * RET_V:(half + 1) * RET_V]
                    d = oh - jnp.mean(oh, axis=-1, keepdims=True)
                    oh = d * lax.rsqrt(jnp.mean(d * d, axis=-1, keepdims=True) + EPS)
                    hs = slice(SSD_WIDTH + h * RET_V, SSD_WIDTH + (h + 1) * RET_V)
                    gate = gs_ref[0, rows(j), h * RET_V:(h + 1) * RET_V].astype(F32)
                    out_ref[0, rows(j), hs] = (oh * gate).astype(out_ref.dtype)


def _mixer(zs, xact, qa, ka, va, gs, dt, tabs):
    rows, length, _ = xact.shape
    nc = length // CHUNK
    group = CHUNKS_PER_STEP if nc % CHUNKS_PER_STEP == 0 else 1
    nb = nc // group
    tb = group * CHUNK

    def bidx(ph, b):
        return b + (1 - ph) * (nb - 1 - 2 * b)

    both = lambda width: pl.BlockSpec((1, tb, width), lambda r, ph, b: (r, bidx(ph, b), 0))
    fwd_only = lambda width: pl.BlockSpec((1, tb, width), lambda r, ph, b: (r, ph * b, 0))
    in_specs = [both(CONV_CH), both(DT_W), both(QK_W), both(RET_WIDTH),
                fwd_only(SSD_WIDTH), fwd_only(QK_W), fwd_only(RET_WIDTH)]
    in_specs += [_const_spec(t.shape) for t in tabs]
    scratch = [
        pltpu.VMEM((SSD_GROUPS, SSD_STATE, GROUP_W), F32),
        pltpu.VMEM((SSD_GROUPS, SSD_STATE, GROUP_W), F32),
        pltpu.VMEM((RET_PAIRS, 2 * RET_QK, 2 * RET_V), F32),
        pltpu.VMEM((RET_PAIRS, 2 * RET_QK, 2 * RET_V), F32),
        pltpu.VMEM((nc, SSD_GROUPS, SSD_STATE, GROUP_W), BF16),
        pltpu.VMEM((nc, RET_PAIRS, 2 * RET_QK, 2 * RET_V), BF16),
    ]
    return pl.pallas_call(
        functools.partial(_mixer_kernel, nb=nb, group=group),
        grid=(rows, 2, nb),
        in_specs=in_specs,
        out_specs=pl.BlockSpec((1, tb, MIX_WIDTH), lambda r, ph, b: (r, ph * b, 0)),
        out_shape=jax.ShapeDtypeStruct((rows, length, MIX_WIDTH), BF16),
        scratch_shapes=scratch,
        compiler_params=pltpu.CompilerParams(
            dimension_semantics=("arbitrary", "arbitrary", "arbitrary"), vmem_limit_bytes=VMEM_LIMIT),
        name="mixer",
    )(xact, dt, ka, va, zs, qa, gs, *tabs)


FF_CHUNK = 512


def _ffn_kernel(mix_ref, h_ref, g_post_ref, g_pre_ref, g_fpost_ref, wo_ref, wg_ref, wu_ref, wd_ref, out_ref):
    m = _dot(mix_ref[...], wo_ref[...])
    h1 = h_ref[...] + _rms(m) * g_post_ref[...]
    f = (_rms(h1) * g_pre_ref[...]).astype(BF16)
    acc = jnp.zeros(h1.shape, F32)
    for c0 in range(0, D_FF, FF_CHUNK):
        c1 = min(c0 + FF_CHUNK, D_FF)
        gt = _dot(f, wg_ref[:, c0:c1])
        up = _dot(f, wu_ref[:, c0:c1])
        acc = acc + _dot((_silu(gt) * up).astype(BF16), wd_ref[c0:c1, :])
    out_ref[...] = h1 + _rms(acc) * g_fpost_ref[...]


def _ffn(mix, h, g_post, g_pre, g_fpost, wo, wg, wu, wd):
    n = h.shape[0]
    tm = _pick_tile(n, (512, 256, 128))
    tok = lambda width: pl.BlockSpec((tm, width), lambda i: (i, 0))
    return pl.pallas_call(
        _ffn_kernel,
        grid=(n // tm,),
        in_specs=[tok(MIX_WIDTH), tok(D_MODEL), _const_spec((1, D_MODEL)), _const_spec((1, D_MODEL)),
                  _const_spec((1, D_MODEL)), _const_spec(wo.shape), _const_spec(wg.shape),
                  _const_spec(wu.shape), _const_spec(wd.shape)],
        out_specs=tok(D_MODEL),
        out_shape=jax.ShapeDtypeStruct((n, D_MODEL), F32),
        compiler_params=pltpu.CompilerParams(dimension_semantics=("arbitrary",), vmem_limit_bytes=VMEM_LIMIT),
        name="ffn",
    )(mix, h, g_post, g_pre, g_fpost, wo, wg, wu, wd)


def _rope_tables(length):
    pos = jnp.arange(length, dtype=F32)
    inv_freq = ROPE_BASE ** (-jnp.arange(0, RET_QK, 2, dtype=F32) / RET_QK)
    ang = pos[:, None] * inv_freq[None, :]
    cos, sin = jnp.cos(ang), jnp.sin(ang)
    cos_full = jnp.tile(jnp.concatenate([cos, cos], axis=1), (1, RET_HEADS))
    sin_full = jnp.tile(jnp.concatenate([-sin, sin], axis=1), (1, RET_HEADS))
    return cos_full, sin_full


def _selection_matrix():
    j = jnp.arange(DT_W)[:, None]
    lane = jnp.arange(2 * SSD_WIDTH)[None, :]
    src = jnp.where(lane < SSD_WIDTH, lane // SSD_HEAD_DIM, SSD_HEADS + (lane - SSD_WIDTH) // SSD_HEAD_DIM)
    sel = (j == src).astype(BF16)
    return jnp.concatenate([sel, sel], axis=0)


def _retention_tables(log_decay):
    lg_f = log_decay[0].astype(F32)
    lg_b = log_decay[1].astype(F32)
    idx = jnp.arange(CHUNK, dtype=F32)
    rel = idx[:, None] - idx[None, :]
    causal = rel >= 0
    d_f = jnp.exp(jnp.where(causal, rel, 0.0)[None] * lg_f[:, None, None])
    d_b = jnp.exp(jnp.where(causal, 0.0, -rel)[None] * lg_b[:, None, None])
    dcomb = jnp.where(causal[None], d_f, d_b)
    dcomb = dcomb.reshape(RET_PAIRS, 2, CHUNK, CHUNK).transpose(0, 2, 1, 3).reshape(RET_PAIRS, CHUNK, 2 * CHUNK)
    per_lane = lambda t: jnp.repeat(t, RET_QK, axis=1)
    xi_f = per_lane(jnp.exp((idx + 1.0)[:, None] * lg_f))
    xi_b = per_lane(jnp.exp((CHUNK - idx)[:, None] * lg_b))
    zeta_f = per_lane(jnp.exp((CHUNK - 1.0 - idx)[:, None] * lg_f))
    zeta_b = per_lane(jnp.exp(idx[:, None] * lg_b))
    xi = jnp.stack([xi_f, xi_b]).astype(BF16)
    zeta = jnp.stack([zeta_f, zeta_b]).astype(BF16)
    gch = jnp.exp(CHUNK * jnp.stack([lg_f, lg_b]))
    gch = jnp.repeat(gch, RET_QK, axis=1).reshape(2, RET_PAIRS, 2 * RET_QK, 1)
    gch = jnp.broadcast_to(gch, (2, RET_PAIRS, 2 * RET_QK, RET_V))
    return dcomb, xi, zeta, gch


def _pad_cols(a, width):
    return jnp.pad(a, ((0, 0), (0, width - a.shape[1])))


def _mixer_tables(a_log, d_skip, ssd_norm, ret_log_decay, sel, tri):
    alog = _pad_cols(a_log.astype(F32).reshape(1, 2 * SSD_HEADS), DT_W)
    dskip = jnp.repeat(d_skip.astype(F32), SSD_HEAD_DIM)[None, :]
    snorm = ssd_norm.astype(F32)[None, :]
    dcomb, xi, zeta, gch = _retention_tables(ret_log_decay)
    return (alog, dskip, snorm, dcomb, xi, zeta, gch, sel, tri)


def _reorder_w_in(w):
    z, xbc, dt, q, k, v, g = jnp.split(
        w, [SSD_WIDTH, SSD_WIDTH + CONV_CH, SSD_WIDTH + CONV_CH + 2 * SSD_HEADS,
            SSD_WIDTH + CONV_CH + 2 * SSD_HEADS + QK_W, SSD_WIDTH + CONV_CH + 2 * SSD_HEADS + 2 * QK_W,
            SSD_WIDTH + CONV_CH + 2 * SSD_HEADS + 2 * QK_W + RET_WIDTH], axis=1)
    return jnp.concatenate([z, xbc, q, k, v, g, _pad_cols(dt, DT_W)], axis=1).astype(BF16)


def kernel(x_prompt, x_sample, meta_tokens, norm_mix_pre, norm_mix_post, norm_ffn_pre, norm_ffn_post, w_in, conv_w, conv_b, dt_bias, a_log, d_skip, ssd_norm, ret_log_decay, w_out, w_gate, w_up, w_down):
    depth = w_in.shape[0]
    x = jnp.concatenate([x_prompt, x_sample], axis=0)
    rows, seq, _ = x.shape
    lead = jnp.concatenate([jnp.zeros((rows, PAD, D_MODEL), x.dtype),
                            jnp.broadcast_to(meta_tokens.astype(x.dtype), (rows, N_META, D_MODEL))], axis=1)
    h = jnp.concatenate([lead, x], axis=1)
    length = h.shape[1]
    cos, sin = _rope_tables(length)
    sel = _selection_matrix()
    idx = jnp.arange(CHUNK)
    tri = (idx[:, None] >= idx[None, :]).astype(BF16)
    row = lambda a: a.astype(F32)[None, :]
    for i in range(depth):
        cw = jnp.pad(conv_w[i].astype(F32), ((0, HALO - CONV_WIDTH), (0, 0)))
        dtb = _pad_cols(dt_bias[i].astype(F32).reshape(1, 2 * SSD_HEADS), DT_W)
        zs, xact, qa, ka, va, gs, dt = _inproj(h, row(norm_mix_pre[i]), _reorder_w_in(w_in[i]), cos, sin,
                                               cw, row(conv_b[i]), dtb)
        tabs = _mixer_tables(a_log[i], d_skip[i], ssd_norm[i], ret_log_decay[i], sel, tri)
        mix = _mixer(zs, xact, qa, ka, va, gs, dt, tabs)
        h = _ffn(mix.reshape(rows * length, MIX_WIDTH), h.reshape(rows * length, D_MODEL),
                 row(norm_mix_post[i]), row(norm_ffn_pre[i]), row(norm_ffn_post[i]),
                 w_out[i].astype(BF16), w_gate[i].astype(BF16), w_up[i].astype(BF16), w_down[i].astype(BF16))
        h = h.reshape(rows, length, D_MODEL)
    y = h[:, PAD + N_META:]
    nb = x_prompt.shape[0]
    return (y[:nb], y[nb:])
```
